```python
import math
import jax, jax.numpy as jnp
from jax import lax
import numpy as np

D_MODEL = 1024
BATCH = 2
SEQ = 8192
DEPTH = 1
DEC_BATCH = 8
DEC_SEQ = 4096
PAST_LEN = 128

MEM_LEN = 256
RMS_EPS = 1e-6
ROPE_THETA = 10000.0
Q_BLOCK = 128
DA_HEADS = 4
DA_QK_DIM = 64
DA_V_DIM = 128
DA_WIDTH = DA_HEADS * DA_V_DIM
GLA_HEADS = 4
GLA_K_DIM = 64
GLA_V_DIM = 128
GLA_WIDTH = GLA_HEADS * GLA_V_DIM
GLA_GATE_RANK = 16
GLA_GATE_NORM = 16.0
GLA_CHUNK = 64
MIX_WIDTH = DA_WIDTH + GLA_WIDTH
XA_HEADS = 4
XA_HEAD_DIM = 128
XA_WIDTH = XA_HEADS * XA_HEAD_DIM
N_EXPERTS = 32
TOP_K = 4
D_FF = D_MODEL
SWIGLU_LIMIT = 7.0
SWIGLU_ALPHA = 1.702
MOE_BLOCK = 128
IN_SIZES = (DA_HEADS * 2 * DA_QK_DIM, DA_HEADS * 2 * DA_QK_DIM, DA_WIDTH,
            GLA_HEADS * GLA_K_DIM, GLA_HEADS * GLA_K_DIM, GLA_WIDTH, GLA_WIDTH,
            GLA_GATE_RANK, GLA_GATE_RANK)
IN_WIDTH = sum(IN_SIZES)

kernel_name = 'hybrid_diffattn_gla_moe_encoder'


def rmsnorm(x, g):
    xf = x.astype(jnp.float32)
    y = xf * lax.rsqrt(jnp.mean(xf * xf, axis=-1, keepdims=True) + RMS_EPS)
    return (y * g.astype(jnp.float32)).astype(x.dtype)


def rope(x, pos):
    d = x.shape[-1]
    half = d // 2
    freqs = ROPE_THETA ** (-jnp.arange(half, dtype=jnp.float32) / half)
    ang = pos.astype(jnp.float32)[:, None] * freqs[None, :]
    cos, sin = jnp.cos(ang), jnp.sin(ang)
    xf = x.astype(jnp.float32)
    x1, x2 = xf[..., :half], xf[..., half:]
    return jnp.concatenate([x1 * cos - x2 * sin, x2 * cos + x1 * sin], axis=-1).astype(x.dtype)


def diff_attention(q, k, v, lam):
    B, H, _, S, dk = q.shape
    dv = v.shape[-1]
    nq = S // Q_BLOCK
    scale = dk ** -0.5
    qb = q.reshape(B, H, 2, nq, Q_BLOCK, dk).transpose(3, 0, 1, 2, 4, 5)

    def block(qblk):
        s = jnp.einsum('bhmqd,bhmkd->bhmqk', qblk, k, preferred_element_type=jnp.float32) * scale
        p = jax.nn.softmax(s, axis=-1)
        a = p[:, :, 0] - lam * p[:, :, 1]
        return jnp.einsum('bhqk,bhkd->bhqd', a.astype(v.dtype), v)

    o = lax.map(block, qb)
    return o.transpose(1, 2, 0, 3, 4).reshape(B, H, S, dv)


def gla_chunked(q, k, v, g):
    B, H, T, dk = q.shape
    dv = v.shape[-1]
    C = GLA_CHUNK
    N = T // C
    q = q.reshape(B, H, N, C, dk)
    k = k.reshape(B, H, N, C, dk)
    v = v.reshape(B, H, N, C, dv)
    G = jnp.cumsum(g.reshape(B, H, N, C, dk), axis=3)
    G_last = G[:, :, :, -1:]
    qg = q * jnp.exp(G)
    kg = k * jnp.exp(-G)
    mask = jnp.tril(jnp.ones((C, C), dtype=bool))
    A = jnp.where(mask, jnp.einsum('bhncd,bhnsd->bhncs', qg, kg), 0.0)
    o_intra = jnp.einsum('bhncs,bhnse->bhnce', A, v)
    kdec = k * jnp.exp(G_last - G)
    dS = jnp.einsum('bhncd,bhnce->bhnde', kdec, v)
    decay = jnp.exp(G_last[:, :, :, 0])

    def step(S, inp):
        d, ds = inp
        return d[..., None] * S + ds, S

    S0 = jnp.zeros((B, H, dk, dv), jnp.float32)
    _, S_prev = lax.scan(step, S0, (decay.transpose(2, 0, 1, 3), dS.transpose(2, 0, 1, 3, 4)))
    S_prev = S_prev.transpose(1, 2, 0, 3, 4)
    o_inter = jnp.einsum('bhncd,bhnde->bhnce', qg, S_prev)
    return (o_intra + o_inter).reshape(B, H, T, dv)


def mem_cross_attention(h, mem, w_xq, w_xkv, g_xq, g_xk, w_xo):
    B, S, _ = h.shape
    M = mem.shape[1]
    q = rmsnorm((h @ w_xq).reshape(B, S, XA_HEADS, XA_HEAD_DIM), g_xq)
    kv = mem @ w_xkv
    k = rmsnorm(kv[..., :XA_WIDTH].reshape(B, M, XA_HEADS, XA_HEAD_DIM), g_xk)
    v = kv[..., XA_WIDTH:].reshape(B, M, XA_HEADS, XA_HEAD_DIM)
    s = jnp.einsum('bqhd,bkhd->bhqk', q, k, preferred_element_type=jnp.float32) * XA_HEAD_DIM ** -0.5
    p = jax.nn.softmax(s, axis=-1).astype(v.dtype)
    o = jnp.einsum('bhqk,bkhd->bqhd', p, v).reshape(B, S, XA_WIDTH)
    return o @ w_xo


def moe_ffn(h, w_router, b_router, w_e_gate, b_e_gate, w_e_up, b_e_up, w_e_down, b_e_down):
    B, S, D = h.shape
    T = B * S
    xt = h.reshape(T, D)
    logits = (xt @ w_router + b_router).astype(jnp.float32)
    top_val, top_idx = lax.top_k(logits, TOP_K)
    gate = jax.nn.softmax(top_val, axis=-1)
    A = T * TOP_K
    e_flat = top_idx.reshape(A)
    tok_flat = jnp.repeat(jnp.arange(T, dtype=jnp.int32), TOP_K)
    w_flat = gate.reshape(A)
    order = jnp.argsort(e_flat)
    e_s, tok_s, w_s = e_flat[order], tok_flat[order], w_flat[order]
    counts = jnp.bincount(e_flat, length=N_EXPERTS)
    padded = ((counts + MOE_BLOCK - 1) // MOE_BLOCK) * MOE_BLOCK
    starts = jnp.cumsum(counts) - counts
    pends = jnp.cumsum(padded)
    pstarts = pends - padded
    dest = pstarts[e_s] + (jnp.arange(A, dtype=jnp.int32) - starts[e_s])
    n_blocks = A // MOE_BLOCK + N_EXPERTS
    P = n_blocks * MOE_BLOCK
    buf_tok = jnp.zeros((P,), jnp.int32).at[dest].set(tok_s)
    buf_w = jnp.zeros((P,), jnp.float32).at[dest].set(w_s)
    block_start = jnp.arange(n_blocks, dtype=jnp.int32) * MOE_BLOCK
    block_exp = jnp.clip(jnp.searchsorted(pends, block_start, side='right'), 0, N_EXPERTS - 1)
    xb = xt[buf_tok].reshape(n_blocks, MOE_BLOCK, D)

    def expert_block(args):
        xblk, e = args
        gt = jnp.minimum(xblk @ w_e_gate[e] + b_e_gate[e], SWIGLU_LIMIT)
        up = jnp.clip(xblk @ w_e_up[e] + b_e_up[e], -SWIGLU_LIMIT, SWIGLU_LIMIT)
        glu = gt * jax.nn.sigmoid(SWIGLU_ALPHA * gt)
        return ((up + 1.0) * glu) @ w_e_down[e] + b_e_down[e]

    yb = lax.map(expert_block, (xb, block_exp)).reshape(P, D)
    yb = yb * buf_w[:, None].astype(yb.dtype)
    y = jnp.zeros((T, D), h.dtype).at[buf_tok].add(yb.astype(h.dtype))
    return y.reshape(B, S, D)


def encoder_layer(x, mem, layer_idx, g_norm_mix, w_in, g_da_q, g_da_k, lam_q1, lam_k1, lam_q2, lam_k2,
                  g_da_subln, w_gla_decay_fwd, b_gla_decay_fwd, w_gla_decay_bwd, b_gla_decay_bwd,
                  g_gla_norm, w_out, g_norm_xattn, g_norm_mem, w_xq, w_xkv, g_xq, g_xk, w_xo,
                  g_norm_ffn, w_router, b_router, w_e_gate, b_e_gate, w_e_up, b_e_up, w_e_down, b_e_down):
    B, S, _ = x.shape
    pos = jnp.arange(S, dtype=jnp.int32)
    h = rmsnorm(x, g_norm_mix)
    z = h @ w_in
    split_points = [int(c) for c in np.cumsum(IN_SIZES)[:-1]]
    da_q, da_k, da_v, gl_q, gl_k, gl_v, gl_g, lr_f, lr_b = jnp.split(z, split_points, axis=-1)

    def heads2(t):
        return t.reshape(B, S, DA_HEADS, 2, DA_QK_DIM).transpose(0, 2, 3, 1, 4)
    q = rope(rmsnorm(heads2(da_q), g_da_q), pos)
    k = rope(rmsnorm(heads2(da_k), g_da_k), pos)
    v = da_v.reshape(B, S, DA_HEADS, DA_V_DIM).transpose(0, 2, 1, 3)
    lam_init = 0.8 - 0.6 * math.exp(-0.3 * layer_idx)
    lam = (jnp.exp(jnp.sum(lam_q1.astype(jnp.float32) * lam_k1.astype(jnp.float32)))
           - jnp.exp(jnp.sum(lam_q2.astype(jnp.float32) * lam_k2.astype(jnp.float32))) + lam_init)
    o_da = rmsnorm(diff_attention(q, k, v, lam), g_da_subln) * (1.0 - lam_init)
    o_da = o_da.transpose(0, 2, 1, 3).reshape(B, S, DA_WIDTH).astype(x.dtype)

    def gheads(t, d):
        return t.reshape(B, S, GLA_HEADS, d).transpose(0, 2, 1, 3).astype(jnp.float32)
    qg = gheads(gl_q, GLA_K_DIM) * GLA_K_DIM ** -0.5
    kg = gheads(gl_k, GLA_K_DIM)
    vg = gheads(gl_v, GLA_V_DIM)
    dec_f = gheads(jax.nn.log_sigmoid((lr_f @ w_gla_decay_fwd + b_gla_decay_fwd).astype(jnp.float32)) / GLA_GATE_NORM, GLA_K_DIM)
    dec_b = gheads(jax.nn.log_sigmoid((lr_b @ w_gla_decay_bwd + b_gla_decay_bwd).astype(jnp.float32)) / GLA_GATE_NORM, GLA_K_DIM)
    o_fwd = gla_chunked(qg, kg, vg, dec_f)
    o_bwd = jnp.flip(gla_chunked(jnp.flip(qg, 2), jnp.flip(kg, 2), jnp.flip(vg, 2), jnp.flip(dec_b, 2)), 2)
    o_gla = rmsnorm(o_fwd + o_bwd, g_gla_norm).transpose(0, 2, 1, 3).reshape(B, S, GLA_WIDTH)
    o_gla = (o_gla * jax.nn.silu(gl_g.astype(jnp.float32))).astype(x.dtype)

    x = x + jnp.concatenate([o_da, o_gla], axis=-1) @ w_out
    x = x + mem_cross_attention(rmsnorm(x, g_norm_xattn), rmsnorm(mem, g_norm_mem), w_xq, w_xkv, g_xq, g_xk, w_xo)
    x = x + moe_ffn(rmsnorm(x, g_norm_ffn), w_router, b_router, w_e_gate, b_e_gate, w_e_up, b_e_up, w_e_down, b_e_down)
    return x


def setup_inputs(seed: int = 0) -> dict:
    key = jax.random.key(seed)
    ks = iter(jax.random.split(key, 48))
    L = DEPTH

    def nrm(shape, scale):
        return jax.random.normal(next(ks), shape, jnp.float32) * scale

    def gain(shape):
        return 1.0 + 0.02 * jax.random.normal(next(ks), shape, jnp.float32)

    return {
        'x_prompt': nrm((BATCH, SEQ, D_MODEL), 1.0),
        'x_sample': nrm((DEC_BATCH, DEC_SEQ, D_MODEL), 1.0),
        'mem_prompt': nrm((BATCH, MEM_LEN, D_MODEL), 1.0),
        'mem_sample': nrm((DEC_BATCH, MEM_LEN, D_MODEL), 1.0),
        'g_norm_mix': gain((L, D_MODEL)),
        'w_in': nrm((L, D_MODEL, IN_WIDTH), D_MODEL ** -0.5),
        'g_da_q': gain((L, DA_QK_DIM)),
        'g_da_k': gain((L, DA_QK_DIM)),
        'lam_q1': nrm((L, DA_QK_DIM), 0.1),
        'lam_k1': nrm((L, DA_QK_DIM), 0.1),
        'lam_q2': nrm((L, DA_QK_DIM), 0.1),
        'lam_k2': nrm((L, DA_QK_DIM), 0.1),
        'g_da_subln': gain((L, DA_V_DIM)),
        'w_gla_decay_fwd': nrm((L, GLA_GATE_RANK, GLA_HEADS * GLA_K_DIM), GLA_GATE_RANK ** -0.5),
        'b_gla_decay_fwd': nrm((L, GLA_HEADS * GLA_K_DIM), 0.1),
        'w_gla_decay_bwd': nrm((L, GLA_GATE_RANK, GLA_HEADS * GLA_K_DIM), GLA_GATE_RANK ** -0.5),
        'b_gla_decay_bwd': nrm((L, GLA_HEADS * GLA_K_DIM), 0.1),
        'g_gla_norm': gain((L, GLA_V_DIM)),
        'w_out': nrm((L, MIX_WIDTH, D_MODEL), MIX_WIDTH ** -0.5),
        'g_norm_xattn': gain((L, D_MODEL)),
        'g_norm_mem': gain((L, D_MODEL)),
        'w_xq': nrm((L, D_MODEL, XA_WIDTH), D_MODEL ** -0.5),
        'w_xkv': nrm((L, D_MODEL, 2 * XA_WIDTH), D_MODEL ** -0.5),
        'g_xq': gain((L, XA_HEAD_DIM)),
        'g_xk': gain((L, XA_HEAD_DIM)),
        'w_xo': nrm((L, XA_WIDTH, D_MODEL), XA_WIDTH ** -0.5),
        'g_norm_ffn': gain((L, D_MODEL)),
        'w_router': nrm((L, D_MODEL, N_EXPERTS), D_MODEL ** -0.5),
        'b_router': nrm((L, N_EXPERTS), 0.01),
        'w_e_gate': nrm((L, N_EXPERTS, D_MODEL, D_FF), D_MODEL ** -0.5),
        'b_e_gate': nrm((L, N_EXPERTS, D_FF), 0.01),
        'w_e_up': nrm((L, N_EXPERTS, D_MODEL, D_FF), D_MODEL ** -0.5),
        'b_e_up': nrm((L, N_EXPERTS, D_FF), 0.01),
        'w_e_down': nrm((L, N_EXPERTS, D_FF, D_MODEL), D_FF ** -0.5),
        'b_e_down': nrm((L, N_EXPERTS, D_MODEL), 0.01),
    }


def reference(x_prompt, x_sample, mem_prompt, mem_sample, g_norm_mix, w_in, g_da_q, g_da_k,
              lam_q1, lam_k1, lam_q2, lam_k2, g_da_subln, w_gla_decay_fwd, b_gla_decay_fwd,
              w_gla_decay_bwd, b_gla_decay_bwd, g_gla_norm, w_out, g_norm_xattn, g_norm_mem,
              w_xq, w_xkv, g_xq, g_xk, w_xo, g_norm_ffn, w_router, b_router,
              w_e_gate, b_e_gate, w_e_up, b_e_up, w_e_down, b_e_down):
    y_prompt = x_prompt
    y_sample = x_sample
    for l in range(DEPTH):
        lw = (g_norm_mix[l], w_in[l], g_da_q[l], g_da_k[l], lam_q1[l], lam_k1[l], lam_q2[l], lam_k2[l],
              g_da_subln[l], w_gla_decay_fwd[l], b_gla_decay_fwd[l], w_gla_decay_bwd[l], b_gla_decay_bwd[l],
              g_gla_norm[l], w_out[l], g_norm_xattn[l], g_norm_mem[l], w_xq[l], w_xkv[l], g_xq[l], g_xk[l],
              w_xo[l], g_norm_ffn[l], w_router[l], b_router[l], w_e_gate[l], b_e_gate[l], w_e_up[l],
              b_e_up[l], w_e_down[l], b_e_down[l])
        y_prompt = encoder_layer(y_prompt, mem_prompt, l, *lw)
        y_sample = encoder_layer(y_sample, mem_sample, l, *lw)
    return (y_prompt, y_sample)
```

```python
import functools
import math

import jax
import jax.numpy as jnp
from jax import lax
from jax.experimental import pallas as pl
from jax.experimental.pallas import tpu as pltpu

F32 = jnp.float32
BF16 = jnp.bfloat16

D_MODEL = 1024
RMS_EPS = 1e-6
ROPE_THETA = 10000.0
HEADS = 4
DA_QK_DIM = 64
GLA_K_DIM = 64
GLA_CHUNK = 64
GLA_GATE_NORM = 16.0
GLA_GATE_RANK = 16
HEAD_V = 128
MEM_LEN = 256
N_EXPERTS = 32
TOP_K = 4
SWIGLU_LIMIT = 7.0
SWIGLU_ALPHA = 1.702

LANES = 128
VMEM_LIMIT = 56 * 1024 * 1024

TM_PROJ = 256
TQ_ATTN = 512
TK_ATTN = 512
GLA_BLOCK = 256
TM_MOE = 256

_NT = (((1,), (1,)), ((), ()))
_TN = (((0,), (0,)), ((), ()))


def _params(semantics):
    return pltpu.CompilerParams(dimension_semantics=semantics, vmem_limit_bytes=VMEM_LIMIT)


def _dot(a, b):
    return jnp.dot(a, b, preferred_element_type=F32)


def _rms(x, g):
    return x * lax.rsqrt(jnp.mean(x * x, axis=-1, keepdims=True) + RMS_EPS) * g


def _split3(x):
    a = x.astype(BF16)
    r = x - a.astype(F32)
    b = r.astype(BF16)
    c = (r - b.astype(F32)).astype(BF16)
    return a, b, c


def _inproj_body(x_ref, gmix_ref, wqk_ref, wv_ref, wg_ref, wlr_ref, gqk_ref, cos_ref, sin_ref, bd_ref,
                 wdec_ref, bdec_ref, tril_ref, triu_ref,
                 q_ref, k_ref, v_ref, gq_ref, gk_ref, gv_ref, gg_ref, gf_ref, gb_ref):
    tm = x_ref.shape[0]
    h = _rms(x_ref[...], gmix_ref[...]).astype(BF16)

    zqk = _dot(h, wqk_ref[...])
    z2 = zqk * zqk
    z2h = z2.astype(BF16)
    z2l = (z2 - z2h.astype(F32)).astype(BF16)
    bd = bd_ref[...]
    ms = jnp.concatenate(
        [_dot(z2h[:, j:j + 256], bd) + _dot(z2l[:, j:j + 256], bd) for j in range(0, 2 * HEADS * LANES, 256)], axis=1)
    y = zqk * lax.rsqrt(ms + RMS_EPS) * gqk_ref[...]
    lane = lax.broadcasted_iota(jnp.int32, (tm, LANES), 1)
    first_half = (lane % DA_QK_DIM) < (DA_QK_DIM // 2)
    low_map = lane < DA_QK_DIM
    cos = cos_ref[...]
    sin = sin_ref[...]
    zero = jnp.zeros((tm, LANES), BF16)
    for j in range(2 * HEADS):
        yh = y[:, j * LANES:(j + 1) * LANES]
        rot = jnp.where(first_half, pltpu.roll(yh, LANES - DA_QK_DIM // 2, 1), pltpu.roll(yh, DA_QK_DIM // 2, 1))
        r = yh * cos + rot * sin
        if j < HEADS:
            rb = (r * (DA_QK_DIM ** -0.5)).astype(BF16)
            q_ref[:, (2 * j) * LANES:(2 * j + 1) * LANES] = jnp.where(low_map, rb, zero)
            q_ref[:, (2 * j + 1) * LANES:(2 * j + 2) * LANES] = jnp.where(low_map, zero, rb)
        else:
            k_ref[:, (j - HEADS) * LANES:(j - HEADS + 1) * LANES] = r.astype(BF16)

    zv = _dot(h, wv_ref[...])
    ones_col = jnp.where(lane == 0, 1.0, 0.0).astype(BF16)
    for j in range(HEADS):
        v_ref[:, (2 * j) * LANES:(2 * j + 1) * LANES] = zv[:, j * LANES:(j + 1) * LANES].astype(BF16)
        v_ref[:, (2 * j + 1) * LANES:(2 * j + 2) * LANES] = ones_col

    zg = _dot(h, wg_ref[...])
    w = HEADS * LANES
    gq_ref[...] = zg[:, 0:w] * (GLA_K_DIM ** -0.5)
    gk_ref[...] = zg[:, w:2 * w]
    gv_ref[...] = zg[:, 2 * w:3 * w].astype(BF16)
    gg_ref[...] = jax.nn.silu(zg[:, 3 * w:4 * w])

    lr = _dot(h, wlr_ref[...]).astype(BF16)
    dec = jax.nn.log_sigmoid(_dot(lr, wdec_ref[...]) + bdec_ref[...]) * (1.0 / GLA_GATE_NORM)
    parts = _split3(dec)
    tril = tril_ref[...]
    triu = triu_ref[...]
    gf_ref[...] = sum(_dot(tril, p[:, 0:w]) for p in parts)
    gb_ref[...] = sum(_dot(triu, p[:, w:2 * w]) for p in parts)


def _inproj(x, seq, wts):
    t = x.shape[0]
    tm = TM_PROJ
    w = HEADS * LANES
    pos_blocks = seq // tm
    const = lambda shape: pl.BlockSpec(shape, lambda i: (0, 0))
    row = lambda width: pl.BlockSpec((tm, width), lambda i: (i, 0))
    outs = [
        jax.ShapeDtypeStruct((t, 2 * w), BF16),
        jax.ShapeDtypeStruct((t, w), BF16),
        jax.ShapeDtypeStruct((t, 2 * w), BF16),
        jax.ShapeDtypeStruct((t, w), F32),
        jax.ShapeDtypeStruct((t, w), F32),
        jax.ShapeDtypeStruct((t, w), BF16),
        jax.ShapeDtypeStruct((t, w), F32),
        jax.ShapeDtypeStruct((t, w), F32),
        jax.ShapeDtypeStruct((t, w), F32),
    ]
    return pl.pallas_call(
        _inproj_body,
        out_shape=outs,
        grid=(t // tm,),
        in_specs=[
            row(D_MODEL), const((1, D_MODEL)), const((D_MODEL, 2 * w)), const((D_MODEL, w)), const((D_MODEL, 4 * w)),
            const((D_MODEL, LANES)), const((1, 2 * w)),
            pl.BlockSpec((tm, LANES), lambda i: (i % pos_blocks, 0)),
            pl.BlockSpec((tm, LANES), lambda i: (i % pos_blocks, 0)),
            const((256, 256)), const((LANES, 2 * w)), const((1, 2 * w)), const((tm, tm)), const((tm, tm)),
        ],
        out_specs=[row(2 * w), row(w), row(2 * w), row(w), row(w), row(w), row(w), row(w), row(w)],
        compiler_params=_params(("parallel",)),
        name="inproj",
    )(x, wts["g_mix"], wts["w_qk"], wts["w_v"], wts["w_g"], wts["w_lr"], wts["g_qk"], wts["cos"][:seq], wts["sin"][:seq],
      wts["bd"], wts["w_dec"], wts["b_dec"], wts["tril"], wts["triu"])


def _dattn_body(lam_ref, q_ref, k_ref, v_ref, gsub_ref, o_ref, m_sc, acc_sc, *, tk, lam_init):
    tq = q_ref.shape[0]
    nk = k_ref.shape[0] // tk
    q2 = jnp.concatenate([q_ref[:, 0:LANES], q_ref[:, LANES:2 * LANES]], axis=0)
    m_sc[...] = jnp.full(m_sc.shape, -jnp.inf, F32)
    acc_sc[...] = jnp.zeros(acc_sc.shape, F32)

    def step(j, carry):
        ks = pl.multiple_of(j * tk, tk)
        k = k_ref[pl.ds(ks, tk), :]
        v = v_ref[pl.ds(ks, tk), :]
        s = lax.dot_general(q2, k, _NT, preferred_element_type=F32)
        m_prev = m_sc[...]
        m_next = jnp.maximum(m_prev, jnp.max(s, axis=1, keepdims=True))
        alpha = jnp.exp(m_prev - m_next)
        p = jnp.exp(s - jnp.concatenate([m_next] * (tk // LANES), axis=1)).astype(BF16)
        acc_sc[...] = acc_sc[...] * jnp.concatenate([alpha, alpha], axis=1) + _dot(p, v)
        m_sc[...] = m_next
        return carry

    lax.fori_loop(0, nk, step, 0)

    lv = lam_ref[...]
    lam = (jnp.exp(jnp.sum(lv[0:1] * lv[1:2], axis=-1, keepdims=True))
           - jnp.exp(jnp.sum(lv[2:3] * lv[3:4], axis=-1, keepdims=True)) + lam_init)
    acc = acc_sc[...]
    a0 = acc[0:tq]
    a1 = acc[tq:2 * tq]
    o = a0[:, 0:LANES] / a0[:, LANES:LANES + 1] - lam * (a1[:, 0:LANES] / a1[:, LANES:LANES + 1])
    o_ref[...] = (_rms(o, gsub_ref[...]) * (1.0 - lam_init)).astype(BF16)


def _dattn(q, k, v, lam_vecs, g_subln, batch, seq, lam_init):
    tq, tk = min(TQ_ATTN, seq), min(TK_ATTN, seq)
    nq = seq // tq
    return pl.pallas_call(
        functools.partial(_dattn_body, tk=tk, lam_init=lam_init),
        out_shape=jax.ShapeDtypeStruct((batch * seq, HEADS * HEAD_V), BF16),
        grid=(batch, HEADS, nq),
        in_specs=[
            pl.BlockSpec((8, LANES), lambda b, h, i: (0, 0)),
            pl.BlockSpec((tq, 2 * LANES), lambda b, h, i: (b * nq + i, h)),
            pl.BlockSpec((seq, LANES), lambda b, h, i: (b, h)),
            pl.BlockSpec((seq, 2 * LANES), lambda b, h, i: (b, h)),
            pl.BlockSpec((1, LANES), lambda b, h, i: (0, 0)),
        ],
        out_specs=pl.BlockSpec((tq, LANES), lambda b, h, i: (b * nq + i, h)),
        scratch_shapes=[pltpu.VMEM((2 * tq, LANES), F32), pltpu.VMEM((2 * tq, 2 * LANES), F32)],
        compiler_params=_params(("parallel", "parallel", "parallel")),
        name="diff_attention",
    )(lam_vecs, q, k, v, g_subln)


def _gla_body(*refs, reverse, final):
    if final:
        gq_ref, gk_ref, gv_ref, g_ref, prev_ref, gate_ref, gnorm_ref, o_ref, st_sc = refs
    else:
        gq_ref, gk_ref, gv_ref, g_ref, o_ref, st_sc = refs
    blk = gq_ref.shape[0]
    nch = blk // GLA_CHUNK

    @pl.when(pl.program_id(1) == 0)
    def _():
        st_sc[...] = jnp.zeros(st_sc.shape, F32)

    r = lax.broadcasted_iota(jnp.int32, (blk, blk), 0)
    c = lax.broadcasted_iota(jnp.int32, (blk, blk), 1)
    same_chunk = (r // GLA_CHUNK) == (c // GLA_CHUNK)
    mask = same_chunk & ((r <= c) if reverse else (r >= c))
    edge = 0 if reverse else GLA_CHUNK - 1
    for h in range(HEADS):
        sl = slice(h * LANES, (h + 1) * LANES)
        g = g_ref[:, sl]
        k = gk_ref[:, sl]
        v = gv_ref[:, sl]
        qg = (gq_ref[:, sl] * jnp.exp(g)).astype(BF16)
        kg = (k * jnp.exp(-g)).astype(BF16)
        a = lax.dot_general(qg, kg, _NT, preferred_element_type=F32)
        o_intra = _dot(jnp.where(mask, a, 0.0).astype(BF16), v)
        st = st_sc[h]
        outs = [None] * nch
        for ci in (range(nch - 1, -1, -1) if reverse else range(nch)):
            rows = slice(ci * GLA_CHUNK, (ci + 1) * GLA_CHUNK)
            gc = g[rows]
            glast = gc[edge:edge + 1]
            kdec = (k[rows] * jnp.exp(glast - gc)).astype(BF16)
            outs[ci] = o_intra[rows] + lax.dot_general(qg[rows], st.astype(BF16), _NT, preferred_element_type=F32)
            st = st * jnp.exp(glast) + lax.dot_general(v[rows], kdec, _TN, preferred_element_type=F32)
        st_sc[h] = st
        o = jnp.concatenate(outs, axis=0)
        if final:
            o_ref[:, sl] = (_rms(prev_ref[:, sl] + o, gnorm_ref[...]) * gate_ref[:, sl]).astype(BF16)
        else:
            o_ref[:, sl] = o


def _gla(gq, gk, gv, g, batch, seq, *, reverse, prev=None, gate=None, g_norm=None):
    final = prev is not None
    blk = GLA_BLOCK
    nblk = seq // blk
    w = HEADS * LANES
    if reverse:
        row = pl.BlockSpec((blk, w), lambda b, i: (b * nblk + nblk - 1 - i, 0))
    else:
        row = pl.BlockSpec((blk, w), lambda b, i: (b * nblk + i, 0))
    ins = [gq, gk, gv, g]
    in_specs = [row, row, row, row]
    if final:
        ins += [prev, gate, g_norm]
        in_specs += [row, row, pl.BlockSpec((1, LANES), lambda b, i: (0, 0))]
    return pl.pallas_call(
        functools.partial(_gla_body, reverse=reverse, final=final),
        out_shape=jax.ShapeDtypeStruct((batch * seq, w), BF16 if final else F32),
        grid=(batch, nblk),
        in_specs=in_specs,
        out_specs=row,
        scratch_shapes=[pltpu.VMEM((HEADS, HEAD_V, LANES), F32)],
        compiler_params=_params(("parallel", "arbitrary")),
        name="gla_bwd" if reverse else "gla_fwd",
    )(*ins)


def _memkv_body(mem_ref, gmem_ref, wkv_ref, gxk_ref, k_ref, v_ref):
    hm = _rms(mem_ref[...], gmem_ref[...]).astype(BF16)
    kv = _dot(hm, wkv_ref[...])
    w = HEADS * HEAD_V
    for h in range(HEADS):
        sl = slice(h * HEAD_V, (h + 1) * HEAD_V)
        k_ref[:, sl] = _rms(kv[:, sl], gxk_ref[...]).astype(BF16)
    v_ref[...] = kv[:, w:2 * w].astype(BF16)


def _memkv(mem, wts):
    rows = mem.shape[0]
    w = HEADS * HEAD_V
    const = lambda shape: pl.BlockSpec(shape, lambda i: (0, 0))
    blk = pl.BlockSpec((MEM_LEN, w), lambda i: (i, 0))
    return pl.pallas_call(
        _memkv_body,
        out_shape=[jax.ShapeDtypeStruct((rows, w), BF16)] * 2,
        grid=(rows // MEM_LEN,),
        in_specs=[pl.BlockSpec((MEM_LEN, D_MODEL), lambda i: (i, 0)), const((1, D_MODEL)), const((D_MODEL, 2 * w)),
                  const((1, HEAD_V))],
        out_specs=[blk, blk],
        compiler_params=_params(("parallel",)),
        name="mem_kv",
    )(mem, wts["g_mem"], wts["w_xkv"], wts["g_xk"])


def _mix_xattn_body(x_ref, oda_ref, ogla_ref, wout_ref, gxa_ref, wxq_ref, gxq_ref, kx_ref, vx_ref, wxo_ref, gffn_ref,
                    wr_ref, br_ref, x2_ref, h3_ref, logit_ref):
    w = HEADS * HEAD_V
    x1 = x_ref[...] + _dot(oda_ref[...], wout_ref[0:w, :]) + _dot(ogla_ref[...], wout_ref[w:2 * w, :])
    qx = _dot(_rms(x1, gxa_ref[...]).astype(BF16), wxq_ref[...])
    heads = []
    for h in range(HEADS):
        sl = slice(h * HEAD_V, (h + 1) * HEAD_V)
        qh = _rms(qx[:, sl], gxq_ref[...]).astype(BF16)
        s = lax.dot_general(qh, kx_ref[:, sl], _NT, preferred_element_type=F32) * (HEAD_V ** -0.5)
        e = jnp.exp(s - jnp.max(s, axis=-1, keepdims=True))
        p = (e / jnp.sum(e, axis=-1, keepdims=True)).astype(BF16)
        heads.append(_dot(p, vx_ref[:, sl]).astype(BF16))
    x2 = x1 + _dot(jnp.concatenate(heads, axis=1), wxo_ref[...])
    x2_ref[...] = x2
    h3 = _rms(x2, gffn_ref[...])
    h3_ref[...] = h3.astype(BF16)
    ha = h3.astype(BF16)
    hb = (h3 - ha.astype(F32)).astype(BF16)
    l1 = _dot(ha, wr_ref[...])
    logit_ref[...] = l1[:, 0:LANES] + (l1[:, LANES:2 * LANES] + _dot(hb, wr_ref[:, 0:LANES])) + br_ref[...]


def _mix_xattn(x, o_da, o_gla, kx, vx, seq, wts):
    t = x.shape[0]
    tm = TM_PROJ
    w = HEADS * HEAD_V
    per_batch = seq // tm
    const = lambda shape: pl.BlockSpec(shape, lambda i: (0,) * len(shape))
    row = lambda width: pl.BlockSpec((tm, width), lambda i: (i, 0))
    memblk = pl.BlockSpec((MEM_LEN, w), lambda i: (i // per_batch, 0))
    return pl.pallas_call(
        _mix_xattn_body,
        out_shape=[jax.ShapeDtypeStruct((t, D_MODEL), F32), jax.ShapeDtypeStruct((t, D_MODEL), BF16),
                   jax.ShapeDtypeStruct((t, LANES), F32)],
        grid=(t // tm,),
        in_specs=[row(D_MODEL), row(w), row(w), const((2 * w, D_MODEL)), const((1, D_MODEL)), const((D_MODEL, w)),
                  const((1, HEAD_V)), memblk, memblk, const((w, D_MODEL)), const((1, D_MODEL)),
                  const((D_MODEL, 2 * LANES)), const((1, LANES))],
        out_specs=[row(D_MODEL), row(D_MODEL), row(LANES)],
        compiler_params=_params(("parallel",)),
        name="mix_xattn_router",
    )(x, o_da, o_gla, wts["w_out"], wts["g_xa"], wts["w_xq"], wts["g_xq"], kx, vx, wts["w_xo"], wts["g_ffn"],
      wts["w_router2"], wts["b_router"])


def _experts_body(tile_expert_ref, n_used_ref, x_ref, wg_ref, bg_ref, wu_ref, bu_ref, wd_ref, bd_ref, y_ref):
    del tile_expert_ref

    @pl.when(pl.program_id(0) < n_used_ref[0])
    def _():
        x = x_ref[...]
        gt = jnp.minimum(_dot(x, wg_ref[...]) + bg_ref[...], SWIGLU_LIMIT)
        up = jnp.clip(_dot(x, wu_ref[...]) + bu_ref[...], -SWIGLU_LIMIT, SWIGLU_LIMIT)
        glu = gt * jax.nn.sigmoid(SWIGLU_ALPHA * gt)
        y_ref[...] = _dot(((up + 1.0) * glu).astype(BF16), wd_ref[...]) + bd_ref[...]


def _experts(xb, tile_expert, n_used, wts):
    p = xb.shape[0]
    tm = TM_MOE
    rows = pl.BlockSpec((tm, D_MODEL), lambda i, te, nu: (jnp.minimum(i, nu[0] - 1), 0))
    wspec = pl.BlockSpec((None, D_MODEL, D_MODEL), lambda i, te, nu: (te[i], 0, 0))
    bspec = pl.BlockSpec((None, 1, D_MODEL), lambda i, te, nu: (te[i], 0, 0))
    return pl.pallas_call(
        _experts_body,
        out_shape=jax.ShapeDtypeStruct((p, D_MODEL), F32),
        grid_spec=pltpu.PrefetchScalarGridSpec(
            num_scalar_prefetch=2,
            grid=(p // tm,),
            in_specs=[rows, wspec, bspec, wspec, bspec, wspec, bspec],
            out_specs=rows,
        ),
        compiler_params=_params(("arbitrary",)),
        name="experts",
    )(tile_expert, n_used, xb, wts["w_e_gate"], wts["b_e_gate"], wts["w_e_up"], wts["b_e_up"], wts["w_e_down"],
      wts["b_e_down"])


def _moe(h3, logits, x2, wts):
    t = h3.shape[0]
    a = t * TOP_K
    top_val, top_idx = lax.top_k(logits[:, :N_EXPERTS], TOP_K)
    gate = jax.nn.softmax(top_val, axis=-1)
    e_flat = top_idx.reshape(a)
    tok_flat = jnp.repeat(jnp.arange(t, dtype=jnp.int32), TOP_K)
    order = jnp.argsort(e_flat)
    e_s, tok_s, w_s = e_flat[order], tok_flat[order], gate.reshape(a)[order]
    counts = jnp.bincount(e_flat, length=N_EXPERTS)
    padded = ((counts + TM_MOE - 1) // TM_MOE) * TM_MOE
    starts = jnp.cumsum(counts) - counts
    pends = jnp.cumsum(padded)
    dest = (pends - padded)[e_s] + (jnp.arange(a, dtype=jnp.int32) - starts[e_s])
    n_tiles = a // TM_MOE + N_EXPERTS
    p = n_tiles * TM_MOE
    buf_tok = jnp.zeros((p,), jnp.int32).at[dest].set(tok_s)
    buf_w = jnp.zeros((p,), F32).at[dest].set(w_s)
    tile_start = jnp.arange(n_tiles, dtype=jnp.int32) * TM_MOE
    tile_expert = jnp.clip(jnp.searchsorted(pends, tile_start, side="right"), 0, N_EXPERTS - 1).astype(jnp.int32)
    n_used = (pends[-1] // TM_MOE).astype(jnp.int32).reshape(1)
    yb = _experts(h3[buf_tok], tile_expert, n_used, wts)
    occupied = jnp.zeros((p,), jnp.bool_).at[dest].set(True)
    return x2 + jnp.zeros((t, D_MODEL), F32).at[buf_tok].add(jnp.where(occupied[:, None], yb * buf_w[:, None], 0.0))


def _pad_heads(w, width):
    lead = w.shape[:-1]
    w = w.reshape(lead + (HEADS, width))
    w = jnp.pad(w, [(0, 0)] * len(lead) + [(0, 0), (0, LANES - width)])
    return w.reshape(lead + (HEADS * LANES,))


def _prepare(layer, max_seq, p):
    l = layer
    w_in = p["w_in"][l]
    cq = HEADS * 2 * DA_QK_DIM
    cv = HEADS * HEAD_V
    ck = HEADS * GLA_K_DIM
    o_k, o_v, o_gq = cq, 2 * cq, 2 * cq + cv
    o_gk, o_gv = o_gq + ck, o_gq + 2 * ck
    o_gg, o_lr = o_gv + cv, o_gv + 2 * cv
    half = DA_QK_DIM // 2
    freqs = ROPE_THETA ** (-jnp.arange(half, dtype=F32) / half)
    ang = jnp.arange(max_seq, dtype=F32)[:, None] * freqs[None, :]
    cos, sin = jnp.cos(ang), jnp.sin(ang)
    r = jnp.arange(TM_PROJ)
    same = (r[:, None] // GLA_CHUNK) == (r[None, :] // GLA_CHUNK)
    blk = jnp.arange(256) // DA_QK_DIM
    w_dec = jnp.zeros((LANES, 2 * HEADS * LANES), F32)
    w_dec = w_dec.at[0:GLA_GATE_RANK, 0:HEADS * LANES].set(_pad_heads(p["w_gla_decay_fwd"][l], GLA_K_DIM))
    w_dec = w_dec.at[GLA_GATE_RANK:2 * GLA_GATE_RANK, HEADS * LANES:].set(_pad_heads(p["w_gla_decay_bwd"][l], GLA_K_DIM))
    w_r = jnp.pad(p["w_router"][l], ((0, 0), (0, LANES - N_EXPERTS)))
    row = lambda v: v.reshape(1, -1)
    return {
        "g_mix": row(p["g_norm_mix"][l]),
        "w_qk": w_in[:, 0:o_v].astype(BF16),
        "w_v": w_in[:, o_v:o_gq].astype(BF16),
        "w_g": jnp.concatenate([_pad_heads(w_in[:, o_gq:o_gk], GLA_K_DIM), _pad_heads(w_in[:, o_gk:o_gv], GLA_K_DIM),
                                w_in[:, o_gv:o_lr]], axis=1).astype(BF16),
        "w_lr": jnp.pad(w_in[:, o_lr:], ((0, 0), (0, LANES - 2 * GLA_GATE_RANK))).astype(BF16),
        "g_qk": row(jnp.concatenate([jnp.tile(p["g_da_q"][l], 2 * HEADS), jnp.tile(p["g_da_k"][l], 2 * HEADS)])),
        "cos": jnp.tile(cos, (1, 4)),
        "sin": jnp.tile(jnp.concatenate([-sin, sin], axis=1), (1, 2)),
        "bd": jnp.where(blk[:, None] == blk[None, :], 1.0 / DA_QK_DIM, 0.0).astype(BF16),
        "w_dec": w_dec.astype(BF16),
        "b_dec": row(jnp.concatenate([_pad_heads(p["b_gla_decay_fwd"][l], GLA_K_DIM),
                                      _pad_heads(p["b_gla_decay_bwd"][l], GLA_K_DIM)])),
        "tril": (same & (r[:, None] >= r[None, :])).astype(BF16),
        "triu": (same & (r[:, None] <= r[None, :])).astype(BF16),
        "lam": jnp.pad(jnp.stack([p["lam_q1"][l], p["lam_k1"][l], p["lam_q2"][l], p["lam_k2"][l]]),
                       ((0, 4), (0, LANES - DA_QK_DIM))),
        "g_subln": row(p["g_da_subln"][l]),
        "g_gla": row(p["g_gla_norm"][l]),
        "w_out": p["w_out"][l].astype(BF16),
        "g_xa": row(p["g_norm_xattn"][l]),
        "g_mem": row(p["g_norm_mem"][l]),
        "w_xq": p["w_xq"][l].astype(BF16),
        "w_xkv": p["w_xkv"][l].astype(BF16),
        "g_xq": row(p["g_xq"][l]),
        "g_xk": row(p["g_xk"][l]),
        "w_xo": p["w_xo"][l].astype(BF16),
        "g_ffn": row(p["g_norm_ffn"][l]),
        "w_router2": jnp.concatenate(_split3(w_r)[0:2], axis=1),
        "b_router": row(jnp.pad(p["b_router"][l], (0, LANES - N_EXPERTS))),
        "w_e_gate": p["w_e_gate"][l].astype(BF16),
        "b_e_gate": p["b_e_gate"][l][:, None, :],
        "w_e_up": p["w_e_up"][l].astype(BF16),
        "b_e_up": p["b_e_up"][l][:, None, :],
        "w_e_down": p["w_e_down"][l].astype(BF16),
        "b_e_down": p["b_e_down"][l][:, None, :],
        "lam_init": 0.8 - 0.6 * math.exp(-0.3 * layer),
    }


def _encoder_layer(x, mem, wts):
    batch, seq, _ = x.shape
    xt = x.reshape(batch * seq, D_MODEL)
    q, k, v, gq, gk, gv, gg, g_fwd, g_bwd = _inproj(xt, seq, wts)
    o_da = _dattn(q, k, v, wts["lam"], wts["g_subln"], batch, seq, wts["lam_init"])
    o_fwd = _gla(gq, gk, gv, g_fwd, batch, seq, reverse=False)
    o_gla = _gla(gq, gk, gv, g_bwd, batch, seq, reverse=True, prev=o_fwd, gate=gg, g_norm=wts["g_gla"])
    kx, vx = _memkv(mem.reshape(batch * MEM_LEN, D_MODEL), wts)
    x2, h3, logits = _mix_xattn(xt, o_da, o_gla, kx, vx, seq, wts)
    return _moe(h3, logits, x2, wts).reshape(batch, seq, D_MODEL)


def kernel(x_prompt, x_sample, mem_prompt, mem_sample, g_norm_mix, w_in, g_da_q, g_da_k, lam_q1, lam_k1, lam_q2, lam_k2, g_da_subln, w_gla_decay_fwd, b_gla_decay_fwd, w_gla_decay_bwd, b_gla_decay_bwd, g_gla_norm, w_out, g_norm_xattn, g_norm_mem, w_xq, w_xkv, g_xq, g_xk, w_xo, g_norm_ffn, w_router, b_router, w_e_gate, b_e_gate, w_e_up, b_e_up, w_e_down, b_e_down):
    p = dict(g_norm_mix=g_norm_mix, w_in=w_in, g_da_q=g_da_q, g_da_k=g_da_k, lam_q1=lam_q1, lam_k1=lam_k1, lam_q2=lam_q2,
             lam_k2=lam_k2, g_da_subln=g_da_subln, w_gla_decay_fwd=w_gla_decay_fwd, b_gla_decay_fwd=b_gla_decay_fwd,
             w_gla_decay_bwd=w_gla_decay_bwd, b_gla_decay_bwd=b_gla_decay_bwd, g_gla_norm=g_gla_norm, w_out=w_out,
             g_norm_xattn=g_norm_xattn, g_norm_mem=g_norm_mem, w_xq=w_xq, w_xkv=w_xkv, g_xq=g_xq, g_xk=g_xk, w_xo=w_xo,
             g_norm_ffn=g_norm_ffn, w_router=w_router, b_router=b_router, w_e_gate=w_e_gate, b_e_gate=b_e_gate,
             w_e_up=w_e_up, b_e_up=b_e_up, w_e_down=w_e_down, b_e_down=b_e_down)
    y_prompt, y_sample = x_prompt, x_sample
    max_seq = max(x_prompt.shape[1], x_sample.shape[1])
    for layer in range(w_in.shape[0]):
        wts = _prepare(layer, max_seq, p)
        y_prompt = _encoder_layer(y_prompt, mem_prompt, wts)
        y_sample = _encoder_layer(y_sample, mem_sample, wts)
    return (y_prompt, y_sample)
```
